```python
import math
import jax, jax.numpy as jnp
from jax import lax
import numpy as np

D_MODEL = 1024
BATCH = 32
SEQ = 2048
DEPTH = 1

CTX_LEN = 256
GRID_W = 64
CHUNK = 2 * GRID_W
ROWS_PER_CHUNK = CHUNK // GRID_W
D_SSM = D_MODEL // 2
SSM_GROUP = 16
N_SSM_GROUPS = D_SSM // SSM_GROUP
SSM_STATE = 64
D_SGU = D_MODEL // 2
N_SGU_GROUPS = 8
SGU_GROUP = D_SGU // N_SGU_GROUPS
D_IN = 2 * D_SSM + 3 * D_SGU + 2 * D_MODEL
DEEPNORM_ALPHA = (2.0 * DEPTH) ** 0.25
DEEPNORM_BETA = (8.0 * DEPTH) ** -0.25
LN_EPS = 1e-6
DT_MIN = 1e-3
DT_MAX = 1e-1

kernel_name = 'hybrid_s5_sgu_parallel_gated_prefix_dit'


def _layer_norm(x):
    x32 = x.astype(jnp.float32)
    mu = jnp.mean(x32, axis=-1, keepdims=True)
    var = jnp.mean(jnp.square(x32 - mu), axis=-1, keepdims=True)
    return ((x32 - mu) * lax.rsqrt(var + LN_EPS)).astype(x.dtype)


def _discretise(lam_re, lam_im, log_dt, b_re, b_im):
    f32 = jnp.float32
    lam_re, lam_im, log_dt = lam_re.astype(f32), lam_im.astype(f32), log_dt.astype(f32)
    b_re, b_im = b_re.astype(f32), b_im.astype(f32)
    dt = jnp.exp(log_dt)[:, None]
    mag = jnp.exp(lam_re * dt)
    ar = mag * jnp.cos(lam_im * dt)
    ai = mag * jnp.sin(lam_im * dt)
    den = lam_re * lam_re + lam_im * lam_im
    nr = ar - 1.0
    fr = (nr * lam_re + ai * lam_im) / den
    fi = (ai * lam_re - nr * lam_im) / den
    bbr = fr[..., None] * b_re - fi[..., None] * b_im
    bbi = fr[..., None] * b_im + fi[..., None] * b_re
    return ar, ai, bbr, bbi


def _complex_affine_combine(e1, e2):
    a1r, a1i, b1r, b1i = e1
    a2r, a2i, b2r, b2i = e2
    return (a1r * a2r - a1i * a2i,
            a1r * a2i + a1i * a2r,
            a2r * b1r - a2i * b1i + b2r,
            a2r * b1i + a2i * b1r + b2i)


def _s5_bidir(u, discs, c_re, c_im, d_skip, h0s, readout):
    bsz, n, _ = u.shape
    ug = u.astype(jnp.float32).reshape(bsz, n, N_SSM_GROUPS, SSM_GROUP)
    y = d_skip.astype(jnp.float32).reshape(N_SSM_GROUPS, SSM_GROUP) * ug if readout else None
    finals = []
    for k, reverse in enumerate((False, True)):
        ar, ai, bbr, bbi = discs[k]
        bu_r = jnp.einsum('blgh,gph->blgp', ug, bbr)
        bu_i = jnp.einsum('blgh,gph->blgp', ug, bbi)
        if h0s is not None:
            h_r, h_i = h0s[k]
            first = n - 1 if reverse else 0
            bu_r = bu_r.at[:, first].add(ar * h_r - ai * h_i)
            bu_i = bu_i.at[:, first].add(ar * h_i + ai * h_r)
        a_r = jnp.broadcast_to(ar, (1, n) + ar.shape)
        a_i = jnp.broadcast_to(ai, (1, n) + ai.shape)
        _, _, s_r, s_i = lax.associative_scan(_complex_affine_combine, (a_r, a_i, bu_r, bu_i),
                                              reverse=reverse, axis=1)
        last = 0 if reverse else n - 1
        finals.append((s_r[:, last], s_i[:, last]))
        if readout:
            y = (y + jnp.einsum('blgp,ghp->blgh', s_r, c_re[k].astype(jnp.float32))
                 - jnp.einsum('blgp,ghp->blgh', s_i, c_im[k].astype(jnp.float32)))
    if readout:
        y = y.reshape(bsz, n, D_SSM)
    return y, finals


def _ssm_glu_gate(y, z, glu_w, glu_b):
    g = jax.nn.gelu(y.astype(z.dtype))
    return g * jax.nn.sigmoid(g @ glu_w + glu_b) * jax.nn.silu(z)


def _sgu_branch(u, v, z, ln_g, ln_b, w_s, b_s, n_chunks):
    bsz, n, _ = u.shape
    u = jax.nn.gelu(u)
    v = _layer_norm(jax.nn.gelu(v)) * ln_g + ln_b
    vg = v.reshape(bsz, n_chunks, CHUNK, N_SGU_GROUPS, SGU_GROUP)
    vm = jnp.einsum('gnm,bkmgc->bkngc', w_s, vg) + b_s.T[None, None, :, :, None]
    return u * vm.reshape(bsz, n, D_SGU) * jax.nn.silu(z)


def _split_proj(p):
    o = np.cumsum([0, D_SSM, D_SSM, D_SGU, D_SGU, D_SGU, D_MODEL, D_MODEL])
    return [p[..., o[j]:o[j + 1]] for j in range(7)]


def _merge_out(ya, yb, ga, gb, w_a, w_b, w_o, b_o):
    m = jax.nn.sigmoid(ga) * (ya @ w_a) + jax.nn.sigmoid(gb) * (yb @ w_b)
    return m @ w_o + b_o


def setup_inputs(seed: int = 0) -> dict:
    key = jax.random.key(seed)
    ks = jax.random.split(key, 32)
    nrm = lambda k, shape, s: jax.random.normal(k, shape, jnp.float32) * s
    G, P, H = N_SSM_GROUPS, SSM_STATE, SSM_GROUP
    lam_im0 = math.pi * jnp.arange(P, dtype=jnp.float32)
    return {
        'x': nrm(ks[0], (BATCH, SEQ, D_MODEL), 1.0),
        'c': nrm(ks[1], (BATCH, D_MODEL), 1.0),
        'ctx': nrm(ks[2], (BATCH, CTX_LEN, D_MODEL), 1.0),
        'c_ctx': nrm(ks[3], (D_MODEL,), 1.0),
        'w_mod': nrm(ks[4], (DEPTH, D_MODEL, 3 * D_MODEL), 0.5 * D_MODEL ** -0.5),
        'b_mod': nrm(ks[5], (DEPTH, 3 * D_MODEL), 0.01),
        'w_in': nrm(ks[6], (DEPTH, D_MODEL, D_IN), D_MODEL ** -0.5),
        'b_in': nrm(ks[7], (DEPTH, D_IN), 0.01),
        'ssm_lam_re': -0.5 + nrm(ks[8], (DEPTH, 2, G, P), 0.01),
        'ssm_lam_im': lam_im0 + nrm(ks[9], (DEPTH, 2, G, P), 0.01),
        'ssm_log_dt': jax.random.uniform(ks[10], (DEPTH, 2, G), jnp.float32,
                                         math.log(DT_MIN), math.log(DT_MAX)),
        'ssm_b_re': nrm(ks[11], (DEPTH, 2, G, P, H), (2.0 * H) ** -0.5),
        'ssm_b_im': nrm(ks[12], (DEPTH, 2, G, P, H), (2.0 * H) ** -0.5),
        'ssm_c_re': nrm(ks[13], (DEPTH, 2, G, H, P), (2.0 * P) ** -0.5),
        'ssm_c_im': nrm(ks[14], (DEPTH, 2, G, H, P), (2.0 * P) ** -0.5),
        'ssm_d': nrm(ks[15], (DEPTH, D_SSM), 1.0),
        'glu_w': nrm(ks[16], (DEPTH, D_SSM, D_SSM), D_SSM ** -0.5),
        'glu_b': nrm(ks[17], (DEPTH, D_SSM), 0.01),
        'sgu_ln_g': 1.0 + nrm(ks[18], (DEPTH, D_SGU), 0.05),
        'sgu_ln_b': nrm(ks[19], (DEPTH, D_SGU), 0.01),
        'sgu_w': nrm(ks[20], (DEPTH, N_SGU_GROUPS, CHUNK, CHUNK), CHUNK ** -0.5),
        'sgu_b': 1.0 + nrm(ks[21], (DEPTH, N_SGU_GROUPS, CHUNK), 0.05),
        'w_branch_a': nrm(ks[22], (DEPTH, D_SSM, D_MODEL), DEEPNORM_BETA * D_SSM ** -0.5),
        'w_branch_b': nrm(ks[23], (DEPTH, D_SGU, D_MODEL), DEEPNORM_BETA * D_SGU ** -0.5),
        'w_out': nrm(ks[24], (DEPTH, D_MODEL, D_MODEL), DEEPNORM_BETA * D_MODEL ** -0.5),
        'b_out': nrm(ks[25], (DEPTH, D_MODEL), 0.01),
        'ln_g': 1.0 + nrm(ks[26], (DEPTH, D_MODEL), 0.05),
        'ln_b': nrm(ks[27], (DEPTH, D_MODEL), 0.01),
    }


def reference(x, c, ctx, c_ctx, w_mod, b_mod, w_in, b_in, ssm_lam_re, ssm_lam_im, ssm_log_dt,
              ssm_b_re, ssm_b_im, ssm_c_re, ssm_c_im, ssm_d, glu_w, glu_b, sgu_ln_g, sgu_ln_b,
              sgu_w, sgu_b, w_branch_a, w_branch_b, w_out, b_out, ln_g, ln_b):
    n_lat = x.shape[1]
    rows = n_lat // GRID_W
    lat_chunks = rows // ROWS_PER_CHUNK
    ctx_chunks = ctx.shape[1] // CHUNK
    for i in range(DEPTH):
        is_last = i == DEPTH - 1
        shift, scale, gate = jnp.split(jax.nn.silu(c) @ w_mod[i] + b_mod[i], 3, axis=-1)
        mod_c = jax.nn.silu(c_ctx) @ w_mod[i] + b_mod[i]
        h = _layer_norm(x) * (1.0 + scale[:, None]) + shift[:, None]
        hc = _layer_norm(ctx) * (1.0 + mod_c[D_MODEL:2 * D_MODEL]) + mod_c[:D_MODEL]

        discs = [_discretise(ssm_lam_re[i, k], ssm_lam_im[i, k], ssm_log_dt[i, k],
                             ssm_b_re[i, k], ssm_b_im[i, k]) for k in range(2)]

        if is_last:
            u_ctx = hc @ w_in[i][:, :D_SSM] + b_in[i][:D_SSM]
            _, ctx_final = _s5_bidir(u_ctx, discs, ssm_c_re[i], ssm_c_im[i], ssm_d[i], None, False)
        else:
            ua_c, za_c, ub_c, vb_c, zb_c, ga_c, gb_c = _split_proj(hc @ w_in[i] + b_in[i])
            y_c, ctx_final = _s5_bidir(ua_c, discs, ssm_c_re[i], ssm_c_im[i], ssm_d[i], None, True)
            ya_c = _ssm_glu_gate(y_c, za_c, glu_w[i], glu_b[i])
            yb_c = _sgu_branch(ub_c, vb_c, zb_c, sgu_ln_g[i], sgu_ln_b[i], sgu_w[i], sgu_b[i], ctx_chunks)
            out_c = _merge_out(ya_c, yb_c, ga_c, gb_c, w_branch_a[i], w_branch_b[i], w_out[i], b_out[i])
            ctx_next = _layer_norm(DEEPNORM_ALPHA * ctx + mod_c[2 * D_MODEL:] * out_c) * ln_g[i] + ln_b[i]

        ua, za, ub, vb, zb, ga, gb = _split_proj(h @ w_in[i] + b_in[i])
        y_a, _ = _s5_bidir(ua, discs, ssm_c_re[i], ssm_c_im[i], ssm_d[i], ctx_final, True)
        ya = _ssm_glu_gate(y_a, za, glu_w[i], glu_b[i])
        yb = _sgu_branch(ub, vb, zb, sgu_ln_g[i], sgu_ln_b[i], sgu_w[i], sgu_b[i], lat_chunks)
        out = _merge_out(ya, yb, ga, gb, w_branch_a[i], w_branch_b[i], w_out[i], b_out[i])
        x = _layer_norm(DEEPNORM_ALPHA * x + gate[:, None] * out) * ln_g[i] + ln_b[i]
        if not is_last:
            ctx = ctx_next
    return x
```

```python
import functools
import math

import jax
import jax.numpy as jnp
from jax import lax
from jax.experimental import pallas as pl
from jax.experimental.pallas import tpu as pltpu

D_MODEL = 1024
GRID_W = 64
CHUNK = 2 * GRID_W
D_SSM = 512
SSM_GROUP = 16
N_SSM_GROUPS = 32
SSM_STATE = 64
D_SGU = 512
N_SGU_GROUPS = 8
SGU_GROUP = 64
DEPTH = 1
DEEPNORM_ALPHA = (2.0 * DEPTH) ** 0.25
LN_EPS = 1e-6

TC = 16
TCH = TC * SSM_GROUP
NST = 2 * SSM_STATE
VMEM_LIMIT = 56 * 1024 * 1024

F32 = jnp.float32
BF16 = jnp.bfloat16
HI = lax.Precision.HIGHEST


def _ln(x):
    mu = jnp.mean(x, axis=-1, keepdims=True)
    xc = x - mu
    var = jnp.mean(xc * xc, axis=-1, keepdims=True)
    return xc * lax.rsqrt(var + LN_EPS)


def _gelu(x):
    return jax.nn.gelu(x)


def _sigmoid(x):
    return jax.nn.sigmoid(x)


def _mod_kernel(c_ref, w_ref, b_ref, o_ref):
    s = c_ref[...]
    s = s * _sigmoid(s)
    o_ref[...] = jnp.dot(s, w_ref[...], precision=HI, preferred_element_type=F32) + b_ref[...]


def _modulation(c_all, w_mod, b_mod):
    rows = c_all.shape[0]
    n = w_mod.shape[1]
    bn = 512
    return pl.pallas_call(
        _mod_kernel,
        grid=(n // bn,),
        in_specs=[pl.BlockSpec((rows, D_MODEL), lambda j: (0, 0)),
                  pl.BlockSpec((D_MODEL, bn), lambda j: (0, j)),
                  pl.BlockSpec((1, bn), lambda j: (0, j))],
        out_specs=pl.BlockSpec((rows, bn), lambda j: (0, j)),
        out_shape=jax.ShapeDtypeStruct((rows, n), F32),
        name="mod",
    )(c_all, w_mod, b_mod.reshape(1, n))


def _proj_u_kernel(x_ref, mod_ref, w_ref, b_ref, o_ref):
    shift = mod_ref[0, 0:1, :]
    scale = mod_ref[0, 1:2, :]
    h = _ln(x_ref[...]) * (1.0 + scale) + shift
    u = jnp.dot(h.astype(BF16), w_ref[...], preferred_element_type=F32) + b_ref[...]
    o_ref[...] = u.astype(BF16)


def _proj_u(x2d, mod, rows_per_mod, w_u, b_u, tm):
    t = x2d.shape[0]
    blocks_per_mod = rows_per_mod // tm
    return pl.pallas_call(
        _proj_u_kernel,
        grid=(t // tm,),
        in_specs=[pl.BlockSpec((tm, D_MODEL), lambda i: (i, 0)),
                  pl.BlockSpec((1, 8, D_MODEL), lambda i: (i // blocks_per_mod, 0, 0)),
                  pl.BlockSpec((D_MODEL, D_SSM), lambda i: (0, 0)),
                  pl.BlockSpec((1, D_SSM), lambda i: (0, 0))],
        out_specs=pl.BlockSpec((tm, D_SSM), lambda i: (i, 0)),
        out_shape=jax.ShapeDtypeStruct((t, D_SSM), BF16),
        compiler_params=pltpu.CompilerParams(dimension_semantics=("arbitrary",),
                                             vmem_limit_bytes=VMEM_LIMIT),
        name="proj_u",
    )(x2d, mod, w_u, b_u)


def _ssm_matrices(lam_re, lam_im, log_dt, b_re, b_im, c_re, c_im, d_skip):
    G, P, H = N_SSM_GROUPS, SSM_STATE, SSM_GROUP
    dt = jnp.exp(log_dt)[..., None]
    mag = jnp.exp(lam_re * dt)
    ar = mag * jnp.cos(lam_im * dt)
    ai = mag * jnp.sin(lam_im * dt)
    den = lam_re * lam_re + lam_im * lam_im
    nr = ar - 1.0
    fr = (nr * lam_re + ai * lam_im) / den
    fi = (ai * lam_re - nr * lam_im) / den
    bbr = fr[..., None] * b_re - fi[..., None] * b_im
    bbi = fr[..., None] * b_im + fi[..., None] * b_re
    tau = jnp.arange(TC + 1, dtype=F32)
    pmag = jnp.exp((lam_re * dt)[..., None] * tau)
    pang = (lam_im * dt)[..., None] * tau
    pr = pmag * jnp.cos(pang)
    pi_ = pmag * jnp.sin(pang)
    er = pr[..., None] * bbr[:, :, :, None, :] - pi_[..., None] * bbi[:, :, :, None, :]
    ei = pr[..., None] * bbi[:, :, :, None, :] + pi_[..., None] * bbr[:, :, :, None, :]
    k = (jnp.einsum('dgxp,dgpth->dgthx', c_re, er, precision=HI)
         - jnp.einsum('dgxp,dgpth->dgthx', c_im, ei, precision=HI))
    i_idx = jnp.arange(TC)[:, None]
    j_idx = jnp.arange(TC)[None, :]
    diff = j_idx - i_idx
    kf = jnp.where((diff >= 0)[None, :, :, None, None], k[0][:, jnp.clip(diff, 0, TC - 1)], 0.0)
    kr = jnp.where((diff <= 0)[None, :, :, None, None], k[1][:, jnp.clip(-diff, 0, TC - 1)], 0.0)
    dmat = (jnp.eye(TC, dtype=F32)[None, :, :, None, None]
            * (jnp.eye(H, dtype=F32)[None] * d_skip.reshape(G, 1, H))[:, None, None, :, :])
    mi = (kf + kr + dmat).transpose(0, 1, 3, 2, 4).reshape(G, TCH, TCH)

    def state_cols(e, rev_time):
        e = e[:, :, :TC, :]
        if rev_time:
            e = e[:, :, ::-1, :]
        return e.transpose(0, 2, 3, 1).reshape(G, TCH, P)

    m = jnp.concatenate([mi,
                         state_cols(er[0], True), state_cols(er[1], False),
                         state_cols(ei[0], True), state_cols(ei[1], False)], axis=-1)
    qr = c_re[..., None] * pr[:, :, None, :, :] - c_im[..., None] * pi_[:, :, None, :, :]
    qi = c_re[..., None] * pi_[:, :, None, :, :] + c_im[..., None] * pr[:, :, None, :, :]

    def read_rows(q, rev_time):
        q = q[..., TC:0:-1] if rev_time else q[..., 1:TC + 1]
        return q.transpose(0, 2, 3, 1).reshape(G, P, TCH)

    z = jnp.zeros((G, P, TCH), F32)
    r_fwd = jnp.concatenate([read_rows(qr[0], False), z, -read_rows(qi[0], False), z], axis=1)
    r_rev = jnp.concatenate([z, read_rows(qr[1], True), z, -read_rows(qi[1], True)], axis=1)
    r = jnp.concatenate([r_fwd, r_rev], axis=1)
    a_re = jnp.concatenate([pr[0][:, :, TC], pr[1][:, :, TC]], axis=-1)
    a_im = jnp.concatenate([pi_[0][:, :, TC], pi_[1][:, :, TC]], axis=-1)
    a = jnp.concatenate([a_re[:, None, :], a_im[:, None, :], jnp.zeros((G, 6, NST), F32)], axis=1)
    return m.astype(BF16), r.astype(BF16), a


def _ssm_kernel(ul_ref, uc_ref, m_ref, r_ref, a_ref, y_ref, z_ref, zc_ref, sin_ref, *, nb, nk_lat, nk_ctx):
    m = m_ref[0]
    rows_lat = nb * nk_lat
    rblk = 512
    zc_ref[...] = jnp.dot(uc_ref[0], m[:, TCH:], preferred_element_type=F32)
    for c in range(rows_lat // rblk):
        z_ref[c * rblk:(c + 1) * rblk, :] = jnp.dot(ul_ref[0, c * rblk:(c + 1) * rblk, :], m,
                                                     preferred_element_type=F32)
    a_re = a_ref[0, 0:1, :]
    a_im = a_ref[0, 1:2, :]
    fwd_lane = lax.broadcasted_iota(jnp.int32, (nb, NST), 1) < SSM_STATE

    def make_body(src_ref, col0, nk, store):
        def body(i, carry):
            re, im = carry
            rf = pl.ds(pl.multiple_of(i * nb, nb), nb)
            rr = pl.ds(pl.multiple_of((nk - 1 - i) * nb, nb), nb)
            if store:
                st = jnp.concatenate([re, im], axis=-1).astype(BF16)
                sin_ref[rf, 0:2 * NST] = st
                sin_ref[rr, 2 * NST:4 * NST] = st
            s_re = jnp.where(fwd_lane, src_ref[rf, col0:col0 + NST], src_ref[rr, col0:col0 + NST])
            s_im = jnp.where(fwd_lane, src_ref[rf, col0 + NST:col0 + 2 * NST],
                             src_ref[rr, col0 + NST:col0 + 2 * NST])
            return (re * a_re - im * a_im + s_re, re * a_im + im * a_re + s_im)
        return body

    zero = jnp.zeros((nb, NST), F32)
    carry = lax.fori_loop(0, nk_ctx, make_body(zc_ref, 0, nk_ctx, False), (zero, zero))
    lax.fori_loop(0, nk_lat, make_body(z_ref, TCH, nk_lat, True), carry)
    r = r_ref[0]
    for c in range(rows_lat // rblk):
        rows = slice(c * rblk, (c + 1) * rblk)
        y = z_ref[rows, 0:TCH] + jnp.dot(sin_ref[rows, :], r, preferred_element_type=F32)
        y_ref[0, rows, :] = y.astype(BF16)


def _ssm(u_lat, u_ctx, m, r, a, nb):
    g, rows_lat, _ = u_lat.shape
    rows_ctx = u_ctx.shape[1]
    kern = functools.partial(_ssm_kernel, nb=nb, nk_lat=rows_lat // nb, nk_ctx=rows_ctx // nb)
    return pl.pallas_call(
        kern,
        grid=(g,),
        in_specs=[pl.BlockSpec((1, rows_lat, TCH), lambda i: (i, 0, 0)),
                  pl.BlockSpec((1, rows_ctx, TCH), lambda i: (i, 0, 0)),
                  pl.BlockSpec((1, TCH, 2 * TCH), lambda i: (i, 0, 0)),
                  pl.BlockSpec((1, 2 * TCH, TCH), lambda i: (i, 0, 0)),
                  pl.BlockSpec((1, 8, NST), lambda i: (i, 0, 0))],
        out_specs=pl.BlockSpec((1, rows_lat, TCH), lambda i: (i, 0, 0)),
        out_shape=jax.ShapeDtypeStruct((g, rows_lat, TCH), BF16),
        scratch_shapes=[pltpu.VMEM((rows_lat, 2 * TCH), F32),
                        pltpu.VMEM((rows_ctx, TCH), F32),
                        pltpu.VMEM((rows_lat, 4 * NST), BF16)],
        compiler_params=pltpu.CompilerParams(dimension_semantics=("arbitrary",),
                                             vmem_limit_bytes=VMEM_LIMIT),
        name="ssm",
    )(u_lat, u_ctx, m, r, a)


def _main_kernel(x_ref, y_ref, mod_ref, w_ref, b_ref, gluw_ref, glub_ref, sg_ref, sb_ref, ws_ref, bs_ref,
                 wa_ref, wb_ref, wo_ref, bo_ref, lng_ref, lnb_ref, o_ref, *, tm):
    x = x_ref[...]
    shift = mod_ref[0, 0:1, :]
    scale = mod_ref[0, 1:2, :]
    gate = mod_ref[0, 2:3, :]
    h = (_ln(x) * (1.0 + scale) + shift).astype(BF16)

    def proj(lo, hi):
        return jnp.dot(h, w_ref[:, lo:hi], preferred_element_type=F32) + b_ref[:, lo:hi]

    g = _gelu(y_ref[...].astype(F32))
    gl = jnp.dot(g.astype(BF16), gluw_ref[...], preferred_element_type=F32) + glub_ref[...]
    z_a = proj(0, D_SSM)
    ya = g * _sigmoid(gl) * (z_a * _sigmoid(z_a))
    u_b = _gelu(proj(D_SSM, 2 * D_SSM))
    v_b = (_ln(_gelu(proj(2 * D_SSM, 3 * D_SSM))) * sg_ref[...] + sb_ref[...]).astype(BF16)
    lane_group = lax.shift_right_logical(lax.broadcasted_iota(jnp.int32, (CHUNK, D_SGU), 1),
                                         int(math.log2(SGU_GROUP)))
    vms = []
    for c in range(tm // CHUNK):
        full = jnp.dot(ws_ref[...], v_b[c * CHUNK:(c + 1) * CHUNK, :], preferred_element_type=F32)
        vm = bs_ref[...]
        for gi in range(N_SGU_GROUPS):
            vm = vm + jnp.where(lane_group == gi, full[gi * CHUNK:(gi + 1) * CHUNK, :], 0.0)
        vms.append(vm)
    vm = jnp.concatenate(vms, axis=0)
    z_b = proj(3 * D_SSM, 4 * D_SSM)
    yb = u_b * vm * (z_b * _sigmoid(z_b))
    g_a = _sigmoid(proj(4 * D_SSM, 4 * D_SSM + D_MODEL))
    g_b = _sigmoid(proj(4 * D_SSM + D_MODEL, 4 * D_SSM + 2 * D_MODEL))
    mrg = (g_a * jnp.dot(ya.astype(BF16), wa_ref[...], preferred_element_type=F32)
           + g_b * jnp.dot(yb.astype(BF16), wb_ref[...], preferred_element_type=F32))
    out = jnp.dot(mrg.astype(BF16), wo_ref[...], preferred_element_type=F32) + bo_ref[...]
    o_ref[...] = _ln(DEEPNORM_ALPHA * x + gate * out) * lng_ref[...] + lnb_ref[...]


def _main(x2d, y2d, mod, rows_per_mod, w_rest, b_rest, glu_w, glu_b, sg, sb, ws, bs, wa, wb, wo, bo, lng, lnb, tm):
    t = x2d.shape[0]
    blocks_per_mod = rows_per_mod // tm
    n_rest = w_rest.shape[1]
    const = lambda *shape: pl.BlockSpec(shape, lambda i: (0,) * len(shape))
    return pl.pallas_call(
        functools.partial(_main_kernel, tm=tm),
        grid=(t // tm,),
        in_specs=[pl.BlockSpec((tm, D_MODEL), lambda i: (i, 0)),
                  pl.BlockSpec((tm, D_SSM), lambda i: (i, 0)),
                  pl.BlockSpec((1, 8, D_MODEL), lambda i: (i // blocks_per_mod, 0, 0)),
                  const(D_MODEL, n_rest), const(1, n_rest),
                  const(D_SSM, D_SSM), const(1, D_SSM),
                  const(1, D_SGU), const(1, D_SGU),
                  const(N_SGU_GROUPS * CHUNK, CHUNK), const(CHUNK, D_SGU),
                  const(D_SSM, D_MODEL), const(D_SGU, D_MODEL), const(D_MODEL, D_MODEL),
                  const(1, D_MODEL), const(1, D_MODEL), const(1, D_MODEL)],
        out_specs=pl.BlockSpec((tm, D_MODEL), lambda i: (i, 0)),
        out_shape=jax.ShapeDtypeStruct((t, D_MODEL), F32),
        compiler_params=pltpu.CompilerParams(dimension_semantics=("arbitrary",),
                                             vmem_limit_bytes=VMEM_LIMIT),
        name="main",
    )(x2d, y2d, mod, w_rest, b_rest, glu_w, glu_b, sg, sb, ws, bs, wa, wb, wo, bo, lng, lnb)


def _to_group_chunks(u2d, bsz, n):
    u = u2d.reshape(bsz, n // TC, TC, N_SSM_GROUPS, SSM_GROUP)
    return u.transpose(3, 1, 0, 2, 4).reshape(N_SSM_GROUPS, (n // TC) * bsz, TCH)


def _from_group_chunks(y, bsz, n):
    y = y.reshape(N_SSM_GROUPS, n // TC, bsz, TC, SSM_GROUP)
    return y.transpose(2, 1, 3, 0, 4).reshape(bsz * n, D_SSM)


def kernel(x, c, ctx, c_ctx, w_mod, b_mod, w_in, b_in, ssm_lam_re, ssm_lam_im, ssm_log_dt, ssm_b_re, ssm_b_im,
           ssm_c_re, ssm_c_im, ssm_d, glu_w, glu_b, sgu_ln_g, sgu_ln_b, sgu_w, sgu_b, w_branch_a, w_branch_b,
           w_out, b_out, ln_g, ln_b):
    bsz, n_lat, _ = x.shape
    n_ctx = ctx.shape[1]
    i = 0
    pad = (-(bsz + 1)) % 8
    c_all = jnp.concatenate([c, c_ctx[None, :], jnp.zeros((pad, D_MODEL), F32)], axis=0)
    mod = _modulation(c_all, w_mod[i], b_mod[i]).reshape(bsz + 1 + pad, 3, D_MODEL)
    mod = jnp.concatenate([mod, jnp.zeros((bsz + 1 + pad, 5, D_MODEL), F32)], axis=1)
    mod_lat = mod[:bsz]
    mod_ctx = mod[bsz:bsz + 1]

    w_bf = w_in[i].astype(BF16)
    w_u = w_bf[:, :D_SSM]
    b_u = b_in[i][:D_SSM].reshape(1, D_SSM)
    x2d = x.reshape(bsz * n_lat, D_MODEL)
    ctx2d = ctx.reshape(bsz * n_ctx, D_MODEL)
    u_lat = _proj_u(x2d, mod_lat, n_lat, w_u, b_u, 512)
    u_ctx = _proj_u(ctx2d, mod_ctx, bsz * n_ctx, w_u, b_u, 512)

    m, r, a = _ssm_matrices(ssm_lam_re[i], ssm_lam_im[i], ssm_log_dt[i], ssm_b_re[i], ssm_b_im[i],
                            ssm_c_re[i], ssm_c_im[i], ssm_d[i])
    y_g = _ssm(_to_group_chunks(u_lat, bsz, n_lat), _to_group_chunks(u_ctx, bsz, n_ctx), m, r, a, bsz)
    y2d = _from_group_chunks(y_g, bsz, n_lat)

    ws = sgu_w[i].reshape(N_SGU_GROUPS * CHUNK, CHUNK).astype(BF16)
    bs = jnp.repeat(sgu_b[i].T, SGU_GROUP, axis=1)
    row = lambda v: v.reshape(1, -1)
    out = _main(x2d, y2d, mod_lat, n_lat, w_bf[:, D_SSM:], row(b_in[i][D_SSM:]),
                glu_w[i].astype(BF16), row(glu_b[i]), row(sgu_ln_g[i]), row(sgu_ln_b[i]), ws, bs,
                w_branch_a[i].astype(BF16), w_branch_b[i].astype(BF16), w_out[i].astype(BF16),
                row(b_out[i]), row(ln_g[i]), row(ln_b[i]), 256)
    return out.reshape(bsz, n_lat, D_MODEL)
```

```python
import functools
import math

import jax
import jax.numpy as jnp
from jax import lax
from jax.experimental import pallas as pl
from jax.experimental.pallas import tpu as pltpu

D_MODEL = 1024
GRID_W = 64
CHUNK = 2 * GRID_W
D_SSM = 512
SSM_GROUP = 16
N_SSM_GROUPS = 32
SSM_STATE = 64
D_SGU = 512
N_SGU_GROUPS = 8
SGU_GROUP = 64
DEPTH = 1
DEEPNORM_ALPHA = (2.0 * DEPTH) ** 0.25
LN_EPS = 1e-6

LANES = 128
TC = 16
TCH = TC * SSM_GROUP
NST = 2 * SSM_STATE
SLOTS = LANES // SSM_GROUP
ROW_PAD = 8
TM_PROJ = 512
TM_MAIN = 1024
VMEM_LIMIT = 56 * 1024 * 1024

F32 = jnp.float32
BF16 = jnp.bfloat16
HI = lax.Precision.HIGHEST


def _ln(x):
    mu = jnp.mean(x, axis=-1, keepdims=True)
    xc = x - mu
    var = jnp.mean(xc * xc, axis=-1, keepdims=True)
    return xc * lax.rsqrt(var + LN_EPS)


def _gelu(x):
    return jax.nn.gelu(x)


def _sigmoid(x):
    return 0.5 * jnp.tanh(0.5 * x) + 0.5


def _slot_transpose(pieces):
    rows = pieces[0].shape[0]
    slot = lax.shift_right_logical(lax.broadcasted_iota(jnp.int32, (rows, LANES), 1),
                                   int(math.log2(SSM_GROUP)))
    rolled = [p if s == 0 else pltpu.roll(p, SSM_GROUP * s, 1) for s, p in enumerate(pieces)]
    out = []
    for a in range(SLOTS):
        merged = rolled[(-a) % SLOTS]
        for t in range(1, SLOTS):
            merged = jnp.where(slot == t, rolled[(t - a) % SLOTS], merged)
        out.append(merged if a == 0 else pltpu.roll(merged, LANES - SSM_GROUP * a, 1))
    return out


def _mod_kernel(c_ref, w_ref, b_ref, o_ref):
    s = c_ref[...]
    s = s * _sigmoid(s)
    o_ref[...] = jnp.dot(s, w_ref[...], precision=HI, preferred_element_type=F32) + b_ref[...]


def _modulation(c_all, w_mod, b_mod):
    rows = c_all.shape[0]
    n = w_mod.shape[1]
    bn = 512
    return pl.pallas_call(
        _mod_kernel,
        grid=(n // bn,),
        in_specs=[pl.BlockSpec((rows, D_MODEL), lambda j: (0, 0)),
                  pl.BlockSpec((D_MODEL, bn), lambda j: (0, j)),
                  pl.BlockSpec((1, bn), lambda j: (0, j))],
        out_specs=pl.BlockSpec((rows, bn), lambda j: (0, j)),
        out_shape=jax.ShapeDtypeStruct((rows, n), F32),
        name="mod",
    )(c_all, w_mod, b_mod.reshape(1, n))


def _proj_u_kernel(x_ref, mod_ref, w_ref, b_ref, o_ref, us_ref, *, tm):
    shift = mod_ref[0, 0:1, :]
    scale = mod_ref[0, 1:2, :]
    h = _ln(x_ref[...]) * (1.0 + scale) + shift
    u = jnp.dot(h.astype(BF16), w_ref[...], preferred_element_type=F32) + b_ref[...]
    nk = tm // TC
    n_tiles = D_SSM // LANES
    for q in range(n_tiles):
        us_ref[q] = u[:, q * LANES:(q + 1) * LANES]
    for q in range(n_tiles):
        for half in range(TC // SLOTS):
            pieces = [us_ref[q, pl.ds(SLOTS * half + s, nk, stride=TC), :] for s in range(SLOTS)]
            for gl, d in enumerate(_slot_transpose(pieces)):
                o_ref[SLOTS * q + gl, :, half * LANES:(half + 1) * LANES] = d.astype(BF16)


def _proj_u(x2d, mod, rows_per_mod, n_seq, w_u, b_u, tm):
    t = x2d.shape[0]
    bsz = t // n_seq
    blocks_per_mod = rows_per_mod // tm
    blocks_per_seq = n_seq // tm
    nk = tm // TC
    return pl.pallas_call(
        functools.partial(_proj_u_kernel, tm=tm),
        grid=(t // tm,),
        in_specs=[pl.BlockSpec((tm, D_MODEL), lambda i: (i, 0)),
                  pl.BlockSpec((1, 8, D_MODEL), lambda i: (i // blocks_per_mod, 0, 0)),
                  pl.BlockSpec((D_MODEL, D_SSM), lambda i: (0, 0)),
                  pl.BlockSpec((1, D_SSM), lambda i: (0, 0))],
        out_specs=pl.BlockSpec((N_SSM_GROUPS, nk, TCH),
                               lambda i: (0, i % blocks_per_seq, i // blocks_per_seq)),
        out_shape=jax.ShapeDtypeStruct((N_SSM_GROUPS, n_seq // TC, bsz * TCH), BF16),
        scratch_shapes=[pltpu.VMEM((D_SSM // LANES, tm, LANES), F32)],
        compiler_params=pltpu.CompilerParams(dimension_semantics=("arbitrary",),
                                             vmem_limit_bytes=VMEM_LIMIT),
        name="proj_u",
    )(x2d, mod, w_u, b_u)


def _ssm_matrices(lam_re, lam_im, log_dt, b_re, b_im, c_re, c_im, d_skip):
    G, P, H = N_SSM_GROUPS, SSM_STATE, SSM_GROUP
    dt = jnp.exp(log_dt)[..., None]
    mag = jnp.exp(lam_re * dt)
    ar = mag * jnp.cos(lam_im * dt)
    ai = mag * jnp.sin(lam_im * dt)
    den = lam_re * lam_re + lam_im * lam_im
    nr = ar - 1.0
    fr = (nr * lam_re + ai * lam_im) / den
    fi = (ai * lam_re - nr * lam_im) / den
    bbr = fr[..., None] * b_re - fi[..., None] * b_im
    bbi = fr[..., None] * b_im + fi[..., None] * b_re
    ctr = c_re.transpose(0, 1, 3, 2)
    cti = c_im.transpose(0, 1, 3, 2)
    e = jnp.arange(-TC, TC + 1, dtype=F32)
    pmag = jnp.exp((lam_re * dt)[..., None] * e)
    pang = (lam_im * dt)[..., None] * e
    pr = pmag * jnp.cos(pang)
    pi_ = pmag * jnp.sin(pang)

    def powers(d, first, step):
        idx = TC + first + step * jnp.arange(TC)
        return pr[d][:, :, idx], pi_[d][:, :, idx]

    def outer(p, v):
        (p_r, p_i), (v_r, v_i) = p, v
        o_r = p_r[..., None] * v_r[:, :, None, :] - p_i[..., None] * v_i[:, :, None, :]
        o_i = p_r[..., None] * v_i[:, :, None, :] + p_i[..., None] * v_r[:, :, None, :]
        return o_r.reshape(G, P, TCH), o_i.reshape(G, P, TCH)

    bf, br = (bbr[0], bbi[0]), (bbr[1], bbi[1])
    cf, cr = (ctr[0], cti[0]), (ctr[1], cti[1])
    xf = outer(powers(0, 0, -1), bf)
    yf = outer(powers(0, 0, 1), cf)
    xr = outer(powers(1, 0, 1), br)
    yr = outer(powers(1, 0, -1), cr)

    def re_prod(x, y):
        return (jnp.einsum('gpa,gpb->gab', x[0], y[0], precision=HI)
                - jnp.einsum('gpa,gpb->gab', x[1], y[1], precision=HI))

    pos = jnp.arange(TCH) // H
    causal = (pos[:, None] <= pos[None, :])[None]
    anti = (pos[:, None] >= pos[None, :])[None]
    dvec = jnp.tile(d_skip.reshape(G, 1, H), (1, TC, 1)).reshape(G, 1, TCH)
    mi = (jnp.where(causal, re_prod(xf, yf), 0.0) + jnp.where(anti, re_prod(xr, yr), 0.0)
          + jnp.eye(TCH, dtype=F32)[None] * dvec)
    sf = outer(powers(0, TC - 1, -1), bf)
    sr = xr
    tr = lambda v: v.transpose(0, 2, 1)
    m = jnp.concatenate([mi, tr(sf[0]), tr(sr[0]), tr(sf[1]), tr(sr[1])], axis=-1)
    qf = outer(powers(0, 1, 1), cf)
    qr = outer(powers(1, TC, -1), cr)
    z = jnp.zeros((G, P, TCH), F32)
    r = jnp.concatenate([qf[0], z, -qf[1], z, z, qr[0], z, -qr[1]], axis=1)
    a_re = jnp.concatenate([pr[0][:, :, 2 * TC], pr[1][:, :, 2 * TC]], axis=-1)
    a_im = jnp.concatenate([pi_[0][:, :, 2 * TC], pi_[1][:, :, 2 * TC]], axis=-1)
    a = jnp.concatenate([a_re[:, None, :], a_im[:, None, :], jnp.zeros((G, 6, NST), F32)], axis=1)
    return m.astype(BF16), r.astype(BF16), a


def _ssm_kernel(ul_ref, uc_ref, m_ref, r_ref, a_ref, y_ref, zi_ref, zs_ref, zc_ref, sin_ref, *,
                nb, nk_lat, nk_ctx):
    m = m_ref[0]
    pitch_lat = nk_lat + ROW_PAD
    pitch_ctx = nk_ctx + ROW_PAD
    bb = 4

    def chunk_states(u_ref, nk, pitch, dst_ref, keep_intra):
        for b0 in range(0, nb, bb):
            u = jnp.concatenate([u_ref[0, :, b * TCH:(b + 1) * TCH] for b in range(b0, b0 + bb)], axis=0)
            if keep_intra:
                z = jnp.dot(u, m, preferred_element_type=F32)
                zi_ref[b0 * nk:(b0 + bb) * nk, :] = z[:, 0:TCH]
                st = z[:, TCH:]
            else:
                st = jnp.dot(u, m[:, TCH:], preferred_element_type=F32)
            for j in range(bb):
                rows = slice((b0 + j) * pitch, (b0 + j) * pitch + nk)
                dst_ref[0, rows, :] = st[j * nk:(j + 1) * nk, 0:NST]
                dst_ref[1, rows, :] = st[j * nk:(j + 1) * nk, NST:2 * NST]

    chunk_states(uc_ref, nk_ctx, pitch_ctx, zc_ref, False)
    chunk_states(ul_ref, nk_lat, pitch_lat, zs_ref, True)

    a_re = a_ref[0, 0:1, :]
    a_im = a_ref[0, 1:2, :]
    fwd_lane = lax.broadcasted_iota(jnp.int32, (nb, NST), 1) < SSM_STATE

    def make_body(src_ref, nk, pitch, store):
        def body(i, carry):
            re, im = carry
            rf = pl.ds(i, nb, stride=pitch)
            rr = pl.ds(nk - 1 - i, nb, stride=pitch)
            if store:
                sin_ref[0, rf, :] = re
                sin_ref[1, rf, :] = im
                sin_ref[2, rr, :] = re
                sin_ref[3, rr, :] = im
            s_re = jnp.where(fwd_lane, src_ref[0, rf, :], src_ref[0, rr, :])
            s_im = jnp.where(fwd_lane, src_ref[1, rf, :], src_ref[1, rr, :])
            return (re * a_re - im * a_im + s_re, re * a_im + im * a_re + s_im)
        return body

    zero = jnp.zeros((nb, NST), F32)
    carry = lax.fori_loop(0, nk_ctx, make_body(zc_ref, nk_ctx, pitch_ctx, False), (zero, zero))
    lax.fori_loop(0, nk_lat, make_body(zs_ref, nk_lat, pitch_lat, True), carry)

    r = r_ref[0]
    for b0 in range(0, nb, bb):
        s_in = jnp.concatenate(
            [jnp.concatenate([sin_ref[t, b * pitch_lat:b * pitch_lat + nk_lat, :] for t in range(4)], axis=1)
             for b in range(b0, b0 + bb)], axis=0).astype(BF16)
        y = zi_ref[b0 * nk_lat:(b0 + bb) * nk_lat, :] + jnp.dot(s_in, r, preferred_element_type=F32)
        for j in range(bb):
            y_ref[0, :, (b0 + j) * TCH:(b0 + j + 1) * TCH] = y[j * nk_lat:(j + 1) * nk_lat, :].astype(BF16)


def _ssm(u_lat, u_ctx, m, r, a, nb):
    g, nk_lat, width = u_lat.shape
    nk_ctx = u_ctx.shape[1]
    kern = functools.partial(_ssm_kernel, nb=nb, nk_lat=nk_lat, nk_ctx=nk_ctx)
    return pl.pallas_call(
        kern,
        grid=(g,),
        in_specs=[pl.BlockSpec((1, nk_lat, width), lambda i: (i, 0, 0)),
                  pl.BlockSpec((1, nk_ctx, width), lambda i: (i, 0, 0)),
                  pl.BlockSpec((1, TCH, 2 * TCH), lambda i: (i, 0, 0)),
                  pl.BlockSpec((1, 2 * TCH, TCH), lambda i: (i, 0, 0)),
                  pl.BlockSpec((1, 8, NST), lambda i: (i, 0, 0))],
        out_specs=pl.BlockSpec((1, nk_lat, width), lambda i: (i, 0, 0)),
        out_shape=jax.ShapeDtypeStruct((g, nk_lat, width), BF16),
        scratch_shapes=[pltpu.VMEM((nb * nk_lat, TCH), F32),
                        pltpu.VMEM((2, nb * (nk_lat + ROW_PAD), NST), F32),
                        pltpu.VMEM((2, nb * (nk_ctx + ROW_PAD), NST), F32),
                        pltpu.VMEM((4, nb * (nk_lat + ROW_PAD), NST), F32)],
        compiler_params=pltpu.CompilerParams(dimension_semantics=("arbitrary",),
                                             vmem_limit_bytes=VMEM_LIMIT),
        name="ssm",
    )(u_lat, u_ctx, m, r, a)


def _main_kernel(x_ref, y_ref, mod_ref, w_ref, b_ref, gluw_ref, glub_ref, sg_ref, sb_ref, ws_ref, bs_ref,
                 wa_ref, wb_ref, wo_ref, bo_ref, lng_ref, lnb_ref, o_ref, ys_ref, *, tm):
    x = x_ref[...]
    shift = mod_ref[0, 0:1, :]
    scale = mod_ref[0, 1:2, :]
    gate = mod_ref[0, 2:3, :]
    h = (_ln(x) * (1.0 + scale) + shift).astype(BF16)

    def proj(lo, hi):
        return jnp.dot(h, w_ref[:, lo:hi], preferred_element_type=F32) + b_ref[:, lo:hi]

    nk = tm // TC
    n_tiles = D_SSM // LANES
    for q in range(n_tiles):
        for half in range(TC // SLOTS):
            pieces = [y_ref[SLOTS * q + gl, :, half * LANES:(half + 1) * LANES].astype(F32)
                      for gl in range(SLOTS)]
            for s, d in enumerate(_slot_transpose(pieces)):
                ys_ref[q, pl.ds(SLOTS * half + s, nk, stride=TC), :] = d
    y = jnp.concatenate([ys_ref[q] for q in range(n_tiles)], axis=1)

    g = _gelu(y)
    gl = jnp.dot(g.astype(BF16), gluw_ref[...], preferred_element_type=F32) + glub_ref[...]
    z_a = proj(0, D_SSM)
    ya = g * _sigmoid(gl) * (z_a * _sigmoid(z_a))
    u_b = _gelu(proj(D_SSM, 2 * D_SSM))
    v_b = (_ln(_gelu(proj(2 * D_SSM, 3 * D_SSM))) * sg_ref[...] + sb_ref[...]).astype(BF16)
    first_group = lax.broadcasted_iota(jnp.int32, (CHUNK, LANES), 1) < SGU_GROUP
    groups_per_tile = LANES // SGU_GROUP
    vms = []
    for c in range(tm // CHUNK):
        tiles = []
        for p in range(D_SGU // LANES):
            v_tile = v_b[c * CHUNK:(c + 1) * CHUNK, p * LANES:(p + 1) * LANES]
            lo = jnp.dot(ws_ref[groups_per_tile * p], v_tile, preferred_element_type=F32)
            hi = jnp.dot(ws_ref[groups_per_tile * p + 1], v_tile, preferred_element_type=F32)
            tiles.append(jnp.where(first_group, lo, hi))
        vms.append(jnp.concatenate(tiles, axis=1))
    vm = jnp.concatenate(vms, axis=0) + jnp.concatenate([bs_ref[...]] * (tm // CHUNK), axis=0)
    z_b = proj(3 * D_SSM, 4 * D_SSM)
    yb = u_b * vm * (z_b * _sigmoid(z_b))
    g_a = _sigmoid(proj(4 * D_SSM, 4 * D_SSM + D_MODEL))
    g_b = _sigmoid(proj(4 * D_SSM + D_MODEL, 4 * D_SSM + 2 * D_MODEL))
    mrg = (g_a * jnp.dot(ya.astype(BF16), wa_ref[...], preferred_element_type=F32)
           + g_b * jnp.dot(yb.astype(BF16), wb_ref[...], preferred_element_type=F32))
    out = jnp.dot(mrg.astype(BF16), wo_ref[...], preferred_element_type=F32) + bo_ref[...]
    o_ref[...] = _ln(DEEPNORM_ALPHA * x + gate * out) * lng_ref[...] + lnb_ref[...]


def _main(x2d, y_g, mod, n_seq, w_rest, b_rest, glu_w, glu_b, sg, sb, ws, bs, wa, wb, wo, bo, lng, lnb, tm):
    t = x2d.shape[0]
    blocks_per_seq = n_seq // tm
    n_rest = w_rest.shape[1]
    nk = tm // TC
    const = lambda *shape: pl.BlockSpec(shape, lambda i: (0,) * len(shape), pipeline_mode=pl.Buffered(1))
    return pl.pallas_call(
        functools.partial(_main_kernel, tm=tm),
        grid=(t // tm,),
        in_specs=[pl.BlockSpec((tm, D_MODEL), lambda i: (i, 0)),
                  pl.BlockSpec((N_SSM_GROUPS, nk, TCH), lambda i: (0, i % blocks_per_seq, i // blocks_per_seq)),
                  pl.BlockSpec((1, 8, D_MODEL), lambda i: (i // blocks_per_seq, 0, 0)),
                  const(D_MODEL, n_rest), const(1, n_rest),
                  const(D_SSM, D_SSM), const(1, D_SSM),
                  const(1, D_SGU), const(1, D_SGU),
                  const(N_SGU_GROUPS, CHUNK, CHUNK), const(CHUNK, D_SGU),
                  const(D_SSM, D_MODEL), const(D_SGU, D_MODEL), const(D_MODEL, D_MODEL),
                  const(1, D_MODEL), const(1, D_MODEL), const(1, D_MODEL)],
        out_specs=pl.BlockSpec((tm, D_MODEL), lambda i: (i, 0)),
        out_shape=jax.ShapeDtypeStruct((t, D_MODEL), F32),
        scratch_shapes=[pltpu.VMEM((D_SSM // LANES, tm, LANES), F32)],
        compiler_params=pltpu.CompilerParams(dimension_semantics=("arbitrary",),
                                             vmem_limit_bytes=VMEM_LIMIT),
        name="main",
    )(x2d, y_g, mod, w_rest, b_rest, glu_w, glu_b, sg, sb, ws, bs, wa, wb, wo, bo, lng, lnb)


def kernel(x, c, ctx, c_ctx, w_mod, b_mod, w_in, b_in, ssm_lam_re, ssm_lam_im, ssm_log_dt, ssm_b_re, ssm_b_im,
           ssm_c_re, ssm_c_im, ssm_d, glu_w, glu_b, sgu_ln_g, sgu_ln_b, sgu_w, sgu_b, w_branch_a, w_branch_b,
           w_out, b_out, ln_g, ln_b):
    bsz, n_lat, _ = x.shape
    n_ctx = ctx.shape[1]
    i = 0
    pad = (-(bsz + 1)) % 8
    c_all = jnp.concatenate([c, c_ctx[None, :], jnp.zeros((pad, D_MODEL), F32)], axis=0)
    mod = _modulation(c_all, w_mod[i], b_mod[i]).reshape(bsz + 1 + pad, 3, D_MODEL)
    mod = jnp.concatenate([mod, jnp.zeros((bsz + 1 + pad, 5, D_MODEL), F32)], axis=1)
    mod_lat = mod[:bsz]
    mod_ctx = mod[bsz:bsz + 1]

    w_bf = w_in[i].astype(BF16)
    w_u = w_bf[:, :D_SSM]
    b_u = b_in[i][:D_SSM].reshape(1, D_SSM)
    x2d = x.reshape(bsz * n_lat, D_MODEL)
    ctx2d = ctx.reshape(bsz * n_ctx, D_MODEL)
    u_lat = _proj_u(x2d, mod_lat, n_lat, n_lat, w_u, b_u, min(TM_PROJ, n_lat))
    u_ctx = _proj_u(ctx2d, mod_ctx, bsz * n_ctx, n_ctx, w_u, b_u, min(TM_PROJ, n_ctx))

    m, r, a = _ssm_matrices(ssm_lam_re[i], ssm_lam_im[i], ssm_log_dt[i], ssm_b_re[i], ssm_b_im[i],
                            ssm_c_re[i], ssm_c_im[i], ssm_d[i])
    y_g = _ssm(u_lat, u_ctx, m, r, a, bsz)

    ws = sgu_w[i].astype(BF16)
    bs = jnp.repeat(sgu_b[i].T, SGU_GROUP, axis=1)
    row = lambda v: v.reshape(1, -1)
    out = _main(x2d, y_g, mod_lat, n_lat, w_bf[:, D_SSM:], row(b_in[i][D_SSM:]),
                glu_w[i].astype(BF16), row(glu_b[i]), row(sgu_ln_g[i]), row(sgu_ln_b[i]), ws, bs,
                w_branch_a[i].astype(BF16), w_branch_b[i].astype(BF16), w_out[i].astype(BF16),
                row(b_out[i]), row(ln_g[i]), row(ln_b[i]), TM_MAIN)
    return out.reshape(bsz, n_lat, D_MODEL)
```

```python
import functools
import math

import jax
import jax.numpy as jnp
from jax import lax
from jax.experimental import pallas as pl
from jax.experimental.pallas import tpu as pltpu

D_MODEL = 1024
GRID_W = 64
CHUNK = 2 * GRID_W
D_SSM = 512
SSM_GROUP = 16
N_SSM_GROUPS = 32
SSM_STATE = 64
D_SGU = 512
N_SGU_GROUPS = 8
SGU_GROUP = 64
DEPTH = 1
DEEPNORM_ALPHA = (2.0 * DEPTH) ** 0.25
LN_EPS = 1e-6

LANES = 128
SUBLANES = 8
TC = 16
TCH = TC * SSM_GROUP
NST = 2 * SSM_STATE
SLOTS = LANES // SSM_GROUP
ROW_PAD = SUBLANES
TC_PITCH = TC + ROW_PAD
TM_PROJ = 512
TM_MAIN = 1024
VMEM_LIMIT = 56 * 1024 * 1024

F32 = jnp.float32
BF16 = jnp.bfloat16
HI = lax.Precision.HIGHEST


def _ln(x):
    mu = jnp.mean(x, axis=-1, keepdims=True)
    xc = x - mu
    var = jnp.mean(xc * xc, axis=-1, keepdims=True)
    return xc * lax.rsqrt(var + LN_EPS)


def _gelu(x):
    return jax.nn.gelu(x)


def _sigmoid(x):
    return 0.5 * jnp.tanh(0.5 * x) + 0.5


def _slot_transpose(pieces):
    rows = pieces[0].shape[0]
    slot = lax.shift_right_logical(lax.broadcasted_iota(jnp.int32, (rows, LANES), 1),
                                   int(math.log2(SSM_GROUP)))
    rolled = [p if s == 0 else pltpu.roll(p, SSM_GROUP * s, 1) for s, p in enumerate(pieces)]
    out = []
    for a in range(SLOTS):
        merged = rolled[(-a) % SLOTS]
        for t in range(1, SLOTS):
            merged = jnp.where(slot == t, rolled[(t - a) % SLOTS], merged)
        out.append(merged if a == 0 else pltpu.roll(merged, LANES - SSM_GROUP * a, 1))
    return out


def _mod_kernel(c_ref, w_ref, b_ref, o_ref):
    s = c_ref[...]
    s = s * _sigmoid(s)
    o_ref[...] = jnp.dot(s, w_ref[...], precision=HI, preferred_element_type=F32) + b_ref[...]


def _modulation(c_all, w_mod, b_mod):
    rows = c_all.shape[0]
    n = w_mod.shape[1]
    bn = 512
    return pl.pallas_call(
        _mod_kernel,
        grid=(n // bn,),
        in_specs=[pl.BlockSpec((rows, D_MODEL), lambda j: (0, 0)),
                  pl.BlockSpec((D_MODEL, bn), lambda j: (0, j)),
                  pl.BlockSpec((1, bn), lambda j: (0, j))],
        out_specs=pl.BlockSpec((rows, bn), lambda j: (0, j)),
        out_shape=jax.ShapeDtypeStruct((rows, n), F32),
        name="mod",
    )(c_all, w_mod, b_mod.reshape(1, n))


def _proj_u_kernel(x_ref, mod_ref, w_ref, b_ref, *out_and_scratch, tm, emit_h):
    if emit_h:
        o_ref, h_ref, us_ref = out_and_scratch
    else:
        o_ref, us_ref = out_and_scratch
    shift = mod_ref[0, 0:1, :]
    scale = mod_ref[0, 1:2, :]
    h = (_ln(x_ref[...]) * (1.0 + scale) + shift).astype(BF16)
    if emit_h:
        h_ref[...] = h
    u = jnp.dot(h, w_ref[...], preferred_element_type=F32) + b_ref[...]
    nk = tm // TC
    n_tiles = D_SSM // LANES
    for q in range(n_tiles):
        for k in range(nk):
            us_ref[q, k * TC_PITCH:k * TC_PITCH + TC, :] = u[k * TC:(k + 1) * TC, q * LANES:(q + 1) * LANES]
    for q in range(n_tiles):
        for half in range(TC // SLOTS):
            pieces = [us_ref[q, pl.ds(SLOTS * half + s, nk, stride=TC_PITCH), :] for s in range(SLOTS)]
            for gl, d in enumerate(_slot_transpose(pieces)):
                o_ref[SLOTS * q + gl, :, half * LANES:(half + 1) * LANES] = d.astype(BF16)


def _proj_u(x2d, mod, rows_per_mod, n_seq, w_u, b_u, tm, emit_h):
    t = x2d.shape[0]
    bsz = t // n_seq
    blocks_per_mod = rows_per_mod // tm
    blocks_per_seq = n_seq // tm
    nk = tm // TC
    out_specs = [pl.BlockSpec((N_SSM_GROUPS, nk, TCH), lambda i: (0, i % blocks_per_seq, i // blocks_per_seq))]
    out_shape = [jax.ShapeDtypeStruct((N_SSM_GROUPS, n_seq // TC, bsz * TCH), BF16)]
    if emit_h:
        out_specs.append(pl.BlockSpec((tm, D_MODEL), lambda i: (i, 0)))
        out_shape.append(jax.ShapeDtypeStruct((t, D_MODEL), BF16))
    return pl.pallas_call(
        functools.partial(_proj_u_kernel, tm=tm, emit_h=emit_h),
        grid=(t // tm,),
        in_specs=[pl.BlockSpec((tm, D_MODEL), lambda i: (i, 0)),
                  pl.BlockSpec((1, 8, D_MODEL), lambda i: (i // blocks_per_mod, 0, 0)),
                  pl.BlockSpec((D_MODEL, D_SSM), lambda i: (0, 0)),
                  pl.BlockSpec((1, D_SSM), lambda i: (0, 0))],
        out_specs=out_specs,
        out_shape=out_shape,
        scratch_shapes=[pltpu.VMEM((D_SSM // LANES, nk * TC_PITCH, LANES), F32)],
        compiler_params=pltpu.CompilerParams(dimension_semantics=("arbitrary",),
                                             vmem_limit_bytes=VMEM_LIMIT),
        name="proj_u",
    )(x2d, mod, w_u, b_u)


def _ssm_matrices(lam_re, lam_im, log_dt, b_re, b_im, c_re, c_im, d_skip):
    G, P, H = N_SSM_GROUPS, SSM_STATE, SSM_GROUP
    dt = jnp.exp(log_dt)[..., None]
    mag = jnp.exp(lam_re * dt)
    ar = mag * jnp.cos(lam_im * dt)
    ai = mag * jnp.sin(lam_im * dt)
    den = lam_re * lam_re + lam_im * lam_im
    nr = ar - 1.0
    fr = (nr * lam_re + ai * lam_im) / den
    fi = (ai * lam_re - nr * lam_im) / den
    bbr = fr[..., None] * b_re - fi[..., None] * b_im
    bbi = fr[..., None] * b_im + fi[..., None] * b_re
    ctr = c_re.transpose(0, 1, 3, 2)
    cti = c_im.transpose(0, 1, 3, 2)
    e = jnp.arange(-TC, TC + 1, dtype=F32)
    pmag = jnp.exp((lam_re * dt)[..., None] * e)
    pang = (lam_im * dt)[..., None] * e
    pr = pmag * jnp.cos(pang)
    pi_ = pmag * jnp.sin(pang)

    def powers(d, first, step):
        idx = TC + first + step * jnp.arange(TC)
        return pr[d][:, :, idx], pi_[d][:, :, idx]

    def outer(p, v):
        (p_r, p_i), (v_r, v_i) = p, v
        o_r = p_r[..., None] * v_r[:, :, None, :] - p_i[..., None] * v_i[:, :, None, :]
        o_i = p_r[..., None] * v_i[:, :, None, :] + p_i[..., None] * v_r[:, :, None, :]
        return o_r.reshape(G, P, TCH), o_i.reshape(G, P, TCH)

    bf, br = (bbr[0], bbi[0]), (bbr[1], bbi[1])
    cf, cr = (ctr[0], cti[0]), (ctr[1], cti[1])
    xf = outer(powers(0, 0, -1), bf)
    yf = outer(powers(0, 0, 1), cf)
    xr = outer(powers(1, 0, 1), br)
    yr = outer(powers(1, 0, -1), cr)

    def re_prod(x, y):
        return (jnp.einsum('gpa,gpb->gab', x[0], y[0], precision=HI)
                - jnp.einsum('gpa,gpb->gab', x[1], y[1], precision=HI))

    pos = jnp.arange(TCH) // H
    causal = (pos[:, None] <= pos[None, :])[None]
    anti = (pos[:, None] >= pos[None, :])[None]
    dvec = jnp.tile(d_skip.reshape(G, 1, H), (1, TC, 1)).reshape(G, 1, TCH)
    mi = (jnp.where(causal, re_prod(xf, yf), 0.0) + jnp.where(anti, re_prod(xr, yr), 0.0)
          + jnp.eye(TCH, dtype=F32)[None] * dvec).astype(BF16)
    sf = outer(powers(0, TC - 1, -1), bf)
    sr = xr
    ms = jnp.concatenate([sf[0], sr[0], sf[1], sr[1]], axis=1).astype(BF16).transpose(0, 2, 1)
    qf = outer(powers(0, 1, 1), cf)
    qr = outer(powers(1, TC, -1), cr)
    z = jnp.zeros((G, P, TCH), BF16)
    cast = lambda v: v.astype(BF16)
    mr = jnp.concatenate([mi, cast(qf[0]), z, cast(-qf[1]), z, z, cast(qr[0]), z, cast(-qr[1])], axis=1)
    a_re = jnp.concatenate([pr[0][:, :, 2 * TC], pr[1][:, :, 2 * TC]], axis=-1)
    a_im = jnp.concatenate([pi_[0][:, :, 2 * TC], pi_[1][:, :, 2 * TC]], axis=-1)
    a = jnp.concatenate([a_re[:, None, :], a_im[:, None, :], jnp.zeros((G, 6, NST), F32)], axis=1)
    return ms, mr, a


def _ssm_kernel(ul_ref, uc_ref, ms_ref, mr_ref, a_ref, y_ref, zs_ref, zc_ref, sin_ref, *,
                nb, nk_lat, nk_ctx):
    ms = ms_ref[0]
    pitch_lat = nk_lat + ROW_PAD
    pitch_ctx = nk_ctx + ROW_PAD
    pitch_b = nb + ROW_PAD
    bb = 4

    def chunk_states(u_ref, nk, pitch, dst_ref):
        for b0 in range(0, nb, bb):
            u = jnp.concatenate([u_ref[0, :, b * TCH:(b + 1) * TCH] for b in range(b0, b0 + bb)], axis=0)
            st = jnp.dot(u, ms, preferred_element_type=F32)
            for j in range(bb):
                rows = slice((b0 + j) * pitch, (b0 + j) * pitch + nk)
                dst_ref[0, rows, :] = st[j * nk:(j + 1) * nk, 0:NST]
                dst_ref[1, rows, :] = st[j * nk:(j + 1) * nk, NST:2 * NST]

    chunk_states(uc_ref, nk_ctx, pitch_ctx, zc_ref)
    chunk_states(ul_ref, nk_lat, pitch_lat, zs_ref)

    a_re = a_ref[0, 0:1, :]
    a_im = a_ref[0, 1:2, :]
    fwd_lane = lax.broadcasted_iota(jnp.int32, (nb, NST), 1) < SSM_STATE

    def make_body(src_ref, nk, pitch, store):
        def body(i, carry):
            re, im = carry
            rf = pl.ds(i, nb, stride=pitch)
            rr = pl.ds(nk - 1 - i, nb, stride=pitch)
            if store:
                wf = pl.ds(pl.multiple_of(i * pitch_b, SUBLANES), nb)
                wr = pl.ds(pl.multiple_of((nk - 1 - i) * pitch_b, SUBLANES), nb)
                sin_ref[0, wf, :] = re
                sin_ref[1, wf, :] = im
                sin_ref[2, wr, :] = re
                sin_ref[3, wr, :] = im
            s_re = jnp.where(fwd_lane, src_ref[0, rf, :], src_ref[0, rr, :])
            s_im = jnp.where(fwd_lane, src_ref[1, rf, :], src_ref[1, rr, :])
            return (re * a_re - im * a_im + s_re, re * a_im + im * a_re + s_im)
        return body

    zero = jnp.zeros((nb, NST), F32)
    carry = lax.fori_loop(0, nk_ctx, make_body(zc_ref, nk_ctx, pitch_ctx, False), (zero, zero), unroll=4)
    lax.fori_loop(0, nk_lat, make_body(zs_ref, nk_lat, pitch_lat, True), carry, unroll=4)

    mr = mr_ref[0]
    for b0 in range(0, nb, bb):
        lhs = jnp.concatenate(
            [jnp.concatenate([ul_ref[0, :, b * TCH:(b + 1) * TCH]]
                             + [sin_ref[t, pl.ds(b, nk_lat, stride=pitch_b), :].astype(BF16) for t in range(4)],
                             axis=1)
             for b in range(b0, b0 + bb)], axis=0)
        y = jnp.dot(lhs, mr, preferred_element_type=F32)
        for j in range(bb):
            y_ref[0, :, (b0 + j) * TCH:(b0 + j + 1) * TCH] = y[j * nk_lat:(j + 1) * nk_lat, :].astype(BF16)


def _ssm(u_lat, u_ctx, ms, mr, a, nb):
    g, nk_lat, width = u_lat.shape
    nk_ctx = u_ctx.shape[1]
    kern = functools.partial(_ssm_kernel, nb=nb, nk_lat=nk_lat, nk_ctx=nk_ctx)
    return pl.pallas_call(
        kern,
        grid=(g,),
        in_specs=[pl.BlockSpec((1, nk_lat, width), lambda i: (i, 0, 0)),
                  pl.BlockSpec((1, nk_ctx, width), lambda i: (i, 0, 0)),
                  pl.BlockSpec((1, TCH, 2 * NST), lambda i: (i, 0, 0)),
                  pl.BlockSpec((1, TCH + 4 * NST, TCH), lambda i: (i, 0, 0)),
                  pl.BlockSpec((1, 8, NST), lambda i: (i, 0, 0))],
        out_specs=pl.BlockSpec((1, nk_lat, width), lambda i: (i, 0, 0)),
        out_shape=jax.ShapeDtypeStruct((g, nk_lat, width), BF16),
        scratch_shapes=[pltpu.VMEM((2, nb * (nk_lat + ROW_PAD), NST), F32),
                        pltpu.VMEM((2, nb * (nk_ctx + ROW_PAD), NST), F32),
                        pltpu.VMEM((4, nk_lat * (nb + ROW_PAD), NST), F32)],
        compiler_params=pltpu.CompilerParams(dimension_semantics=("arbitrary",),
                                             vmem_limit_bytes=VMEM_LIMIT),
        name="ssm",
    )(u_lat, u_ctx, ms, mr, a)


def _main_kernel(x_ref, h_ref, y_ref, mod_ref, w_ref, b_ref, gluw_ref, glub_ref, sg_ref, sb_ref, ws_ref, bs_ref,
                 wa_ref, wb_ref, wo_ref, bo_ref, lng_ref, lnb_ref, o_ref, ys_ref, *, tm):
    gate = mod_ref[0, 2:3, :]
    h = h_ref[...]

    def proj(lo, hi):
        return jnp.dot(h, w_ref[:, lo:hi], preferred_element_type=F32) + b_ref[:, lo:hi]

    nk = tm // TC
    n_tiles = D_SSM // LANES
    p_v = proj(2 * D_SSM, 3 * D_SSM)
    for q in range(n_tiles):
        for half in range(TC // SLOTS):
            pieces = [y_ref[SLOTS * q + gl, :, half * LANES:(half + 1) * LANES].astype(F32)
                      for gl in range(SLOTS)]
            for s, d in enumerate(_slot_transpose(pieces)):
                ys_ref[q, pl.ds(SLOTS * half + s, nk, stride=TC_PITCH), :] = d
    z_a = proj(0, D_SSM)
    v_b = (_ln(_gelu(p_v)) * sg_ref[...] + sb_ref[...]).astype(BF16)
    p_u = proj(D_SSM, 2 * D_SSM)
    y = jnp.concatenate(
        [jnp.concatenate([ys_ref[q, k * TC_PITCH:k * TC_PITCH + TC, :] for k in range(nk)], axis=0)
         for q in range(n_tiles)], axis=1)
    g = _gelu(y)
    first_group = lax.broadcasted_iota(jnp.int32, (CHUNK, LANES), 1) < SGU_GROUP
    groups_per_tile = LANES // SGU_GROUP
    vms = []
    for c in range(tm // CHUNK):
        tiles = []
        for p in range(D_SGU // LANES):
            v_tile = v_b[c * CHUNK:(c + 1) * CHUNK, p * LANES:(p + 1) * LANES]
            lo = jnp.dot(ws_ref[groups_per_tile * p], v_tile, preferred_element_type=F32)
            hi = jnp.dot(ws_ref[groups_per_tile * p + 1], v_tile, preferred_element_type=F32)
            tiles.append(jnp.where(first_group, lo, hi))
        vms.append(jnp.concatenate(tiles, axis=1))
    gl = jnp.dot(g.astype(BF16), gluw_ref[...], preferred_element_type=F32) + glub_ref[...]
    u_b = _gelu(p_u)
    z_b = proj(3 * D_SSM, 4 * D_SSM)
    ya = (g * _sigmoid(gl) * (z_a * _sigmoid(z_a))).astype(BF16)
    d_a = jnp.dot(ya, wa_ref[...], preferred_element_type=F32)
    vm = jnp.concatenate(vms, axis=0) + jnp.concatenate([bs_ref[...]] * (tm // CHUNK), axis=0)
    yb = (u_b * vm * (z_b * _sigmoid(z_b))).astype(BF16)
    d_b = jnp.dot(yb, wb_ref[...], preferred_element_type=F32)
    p_ga = proj(4 * D_SSM, 4 * D_SSM + D_MODEL)
    p_gb = proj(4 * D_SSM + D_MODEL, 4 * D_SSM + 2 * D_MODEL)
    m_a = _sigmoid(p_ga) * d_a
    mrg = (m_a + _sigmoid(p_gb) * d_b).astype(BF16)
    rh = tm // 2
    for r0 in range(0, tm, rh):
        out = jnp.dot(mrg[r0:r0 + rh, :], wo_ref[...], preferred_element_type=F32) + bo_ref[...]
        o_ref[r0:r0 + rh, :] = (_ln(DEEPNORM_ALPHA * x_ref[r0:r0 + rh, :] + gate * out) * lng_ref[...]
                                + lnb_ref[...])


def _main(x2d, h2d, y_g, mod, n_seq, w_rest, b_rest, glu_w, glu_b, sg, sb, ws, bs, wa, wb, wo, bo, lng, lnb, tm):
    t = x2d.shape[0]
    blocks_per_seq = n_seq // tm
    n_rest = w_rest.shape[1]
    nk = tm // TC
    const = lambda *shape: pl.BlockSpec(shape, lambda i: (0,) * len(shape), pipeline_mode=pl.Buffered(1))
    return pl.pallas_call(
        functools.partial(_main_kernel, tm=tm),
        grid=(t // tm,),
        in_specs=[pl.BlockSpec((tm, D_MODEL), lambda i: (i, 0)),
                  pl.BlockSpec((tm, D_MODEL), lambda i: (i, 0)),
                  pl.BlockSpec((N_SSM_GROUPS, nk, TCH), lambda i: (0, i % blocks_per_seq, i // blocks_per_seq)),
                  pl.BlockSpec((1, 8, D_MODEL), lambda i: (i // blocks_per_seq, 0, 0)),
                  const(D_MODEL, n_rest), const(1, n_rest),
                  const(D_SSM, D_SSM), const(1, D_SSM),
                  const(1, D_SGU), const(1, D_SGU),
                  const(N_SGU_GROUPS, CHUNK, CHUNK), const(CHUNK, D_SGU),
                  const(D_SSM, D_MODEL), const(D_SGU, D_MODEL), const(D_MODEL, D_MODEL),
                  const(1, D_MODEL), const(1, D_MODEL), const(1, D_MODEL)],
        out_specs=pl.BlockSpec((tm, D_MODEL), lambda i: (i, 0)),
        out_shape=jax.ShapeDtypeStruct((t, D_MODEL), F32),
        scratch_shapes=[pltpu.VMEM((D_SSM // LANES, nk * TC_PITCH, LANES), F32)],
        compiler_params=pltpu.CompilerParams(dimension_semantics=("arbitrary",),
                                             vmem_limit_bytes=VMEM_LIMIT),
        name="main",
    )(x2d, h2d, y_g, mod, w_rest, b_rest, glu_w, glu_b, sg, sb, ws, bs, wa, wb, wo, bo, lng, lnb)


def kernel(x, c, ctx, c_ctx, w_mod, b_mod, w_in, b_in, ssm_lam_re, ssm_lam_im, ssm_log_dt, ssm_b_re, ssm_b_im,
           ssm_c_re, ssm_c_im, ssm_d, glu_w, glu_b, sgu_ln_g, sgu_ln_b, sgu_w, sgu_b, w_branch_a, w_branch_b,
           w_out, b_out, ln_g, ln_b):
    bsz, n_lat, _ = x.shape
    n_ctx = ctx.shape[1]
    i = 0
    pad = (-(bsz + 1)) % 8
    c_all = jnp.concatenate([c, c_ctx[None, :], jnp.zeros((pad, D_MODEL), F32)], axis=0)
    mod = _modulation(c_all, w_mod[i], b_mod[i]).reshape(bsz + 1 + pad, 3, D_MODEL)
    mod = jnp.concatenate([mod, jnp.zeros((bsz + 1 + pad, 5, D_MODEL), F32)], axis=1)
    mod_lat = mod[:bsz]
    mod_ctx = mod[bsz:bsz + 1]

    w_bf = w_in[i].astype(BF16)
    w_u = w_bf[:, :D_SSM]
    b_u = b_in[i][:D_SSM].reshape(1, D_SSM)
    x2d = x.reshape(bsz * n_lat, D_MODEL)
    ctx2d = ctx.reshape(bsz * n_ctx, D_MODEL)
    u_lat, h2d = _proj_u(x2d, mod_lat, n_lat, n_lat, w_u, b_u, min(TM_PROJ, n_lat), True)
    (u_ctx,) = _proj_u(ctx2d, mod_ctx, bsz * n_ctx, n_ctx, w_u, b_u, min(TM_PROJ, n_ctx), False)

    ms, mr, a = _ssm_matrices(ssm_lam_re[i], ssm_lam_im[i], ssm_log_dt[i], ssm_b_re[i], ssm_b_im[i],
                              ssm_c_re[i], ssm_c_im[i], ssm_d[i])
    y_g = _ssm(u_lat, u_ctx, ms, mr, a, bsz)

    ws = sgu_w[i].astype(BF16)
    bs = jnp.repeat(sgu_b[i].T, SGU_GROUP, axis=1)
    row = lambda v: v.reshape(1, -1)
    out = _main(x2d, h2d, y_g, mod_lat, n_lat, w_bf[:, D_SSM:], row(b_in[i][D_SSM:]),
                glu_w[i].astype(BF16), row(glu_b[i]), row(sgu_ln_g[i]), row(sgu_ln_b[i]), ws, bs,
                w_branch_a[i].astype(BF16), w_branch_b[i].astype(BF16), w_out[i].astype(BF16),
                row(b_out[i]), row(ln_g[i]), row(ln_b[i]), TM_MAIN)
    return out.reshape(bsz, n_lat, D_MODEL)
```
